```python
import math
import jax, jax.numpy as jnp
from jax import lax
import numpy as np

D_MODEL = 1024
BATCH = 16
SEQ = 2048
DEPTH = 2

N_MIXERS = 2
N_ATTN_LAYERS = (DEPTH + 1) // 2
N_SSM_LAYERS = DEPTH // 2
HEAD_DIM = 64
ATTN_HEADS = D_MODEL // HEAD_DIM
ATTN_GROUPS = ((128, 1), (512, 4), (2048, 16))
N_ATTN_GROUPS = len(ATTN_GROUPS)
ATTN_WIDTH = ATTN_HEADS * HEAD_DIM
QKV_WIDTH = N_ATTN_GROUPS * 3 * ATTN_WIDTH
ALIBI_MAX_BIAS = 8.0
NEG_INF = -1e30
SSM_WIDTH = D_MODEL
SSM_GROUP_CH = 16
SSM_GROUPS = SSM_WIDTH // SSM_GROUP_CH
SSM_STATE = 64
SSM_DIRS = 2
SSM_DT_MIN = 1e-3
SSM_DT_MAX = 1e-1
FFN_HIDDEN = -(-8 * D_MODEL // (3 * 256)) * 256
NORM_EPS = 1e-6

kernel_name = "hybrid_dilated_attn_s5_encoder"


def rms_norm(x, gain):
    xf = x.astype(jnp.float32)
    y = xf * lax.rsqrt(jnp.mean(xf * xf, axis=-1, keepdims=True) + NORM_EPS)
    return (y * gain.astype(jnp.float32)).astype(x.dtype)


def alibi_slopes(n_heads):
    return 2.0 ** (-ALIBI_MAX_BIAS * jnp.arange(1, n_heads + 1, dtype=jnp.float32) / n_heads)


def dilated_band_attention(q, k, v, window, dilation, slopes):
    b, s, h, e = q.shape
    half = window // 2 // dilation
    length = s // dilation
    n_blk = -(-length // half)
    lp = n_blk * half

    def to_residue(t):
        return t.reshape(b, length, dilation, h, e).transpose(0, 2, 1, 3, 4)

    qr, kr, vr = to_residue(q), to_residue(k), to_residue(v)
    qb = jnp.pad(qr, ((0, 0), (0, 0), (0, lp - length), (0, 0), (0, 0)))
    qb = qb.reshape(b, dilation, n_blk, half, h, e)

    def band(t):
        tp = jnp.pad(t, ((0, 0), (0, 0), (half, lp - length + half), (0, 0), (0, 0)))
        tp = tp.reshape(b, dilation, n_blk + 2, half, h, e)
        return jnp.concatenate([tp[:, :, :-2], tp[:, :, 1:-1], tp[:, :, 2:]], axis=3)

    kw, vw = band(kr), band(vr)
    rel = jnp.arange(3 * half)[None, :] - half - jnp.arange(half)[:, None]
    kpos = jnp.arange(n_blk)[:, None] * half + jnp.arange(3 * half)[None, :] - half
    valid = (jnp.abs(rel)[None] <= half) & ((kpos >= 0) & (kpos < length))[:, None, :]
    bias = -slopes[:, None, None] * (jnp.abs(rel) * dilation).astype(jnp.float32)

    scores = jnp.einsum('brnqhe,brnkhe->brnhqk', qb, kw, preferred_element_type=jnp.float32)
    scores = jnp.where(valid[None, None, :, None], scores + bias[None, None, None], NEG_INF)
    m = jnp.max(scores, axis=-1, keepdims=True)
    p = jnp.exp(scores - m)
    den = jnp.sum(p, axis=-1, keepdims=True)
    o = jnp.einsum('brnhqk,brnkhe->brnqhe', p, vw.astype(jnp.float32))
    o = o / den.transpose(0, 1, 2, 4, 3, 5)
    lse = (m + jnp.log(den))[..., 0].transpose(0, 1, 2, 4, 3)
    o = o.reshape(b, dilation, lp, h, e)[:, :, :length].transpose(0, 2, 1, 3, 4).reshape(b, s, h, e)
    lse = lse.reshape(b, dilation, lp, h)[:, :, :length].transpose(0, 2, 1, 3).reshape(b, s, h)
    return o, lse


def dilated_attention_mixer(hn, w_qkv, w_out):
    b, s, _ = hn.shape
    qkv = (hn @ w_qkv).reshape(b, s, N_ATTN_GROUPS, 3, ATTN_HEADS, HEAD_DIM)
    slopes = alibi_slopes(ATTN_HEADS)
    outs, lses = [], []
    for g, (window, dilation) in enumerate(ATTN_GROUPS):
        q = qkv[:, :, g, 0] * (HEAD_DIM ** -0.5)
        o, lse = dilated_band_attention(q, qkv[:, :, g, 1], qkv[:, :, g, 2], window, dilation, slopes)
        outs.append(o)
        lses.append(lse)
    weights = jax.nn.softmax(jnp.stack(lses, axis=0), axis=0)
    o = jnp.sum(weights[..., None] * jnp.stack(outs, axis=0), axis=0)
    return o.reshape(b, s, ATTN_WIDTH).astype(hn.dtype) @ w_out


def complex_affine_combine(first, second):
    a1r, a1i, b1r, b1i = first
    a2r, a2i, b2r, b2i = second
    return (a1r * a2r - a1i * a2i,
            a1r * a2i + a1i * a2r,
            a2r * b1r - a2i * b1i + b2r,
            a2r * b1i + a2i * b1r + b2i)


def s5_direction(u, a_re, a_im, log_dt, b_re, b_im, c_re, c_im, reverse):
    a_re = a_re.astype(jnp.float32)
    a_im = a_im.astype(jnp.float32)
    dt = jnp.exp(log_dt.astype(jnp.float32))[:, None]
    mag = jnp.exp(a_re * dt)
    lr, li = mag * jnp.cos(a_im * dt), mag * jnp.sin(a_im * dt)
    inv = 1.0 / (a_re * a_re + a_im * a_im)
    fr = ((lr - 1.0) * a_re + li * a_im) * inv
    fi = (li * a_re - (lr - 1.0) * a_im) * inv
    b_re = b_re.astype(jnp.float32)
    b_im = b_im.astype(jnp.float32)
    bbr = fr[..., None] * b_re - fi[..., None] * b_im
    bbi = fr[..., None] * b_im + fi[..., None] * b_re
    bur = jnp.einsum('bsgc,gpc->bsgp', u, bbr)
    bui = jnp.einsum('bsgc,gpc->bsgp', u, bbi)
    s = u.shape[1]
    ar = jnp.broadcast_to(lr[None, None], (1, s) + lr.shape)
    ai = jnp.broadcast_to(li[None, None], (1, s) + li.shape)
    _, _, xr, xi = lax.associative_scan(complex_affine_combine, (ar, ai, bur, bui),
                                        reverse=reverse, axis=1)
    return (jnp.einsum('bsgp,gcp->bsgc', xr, c_re.astype(jnp.float32))
            - jnp.einsum('bsgp,gcp->bsgc', xi, c_im.astype(jnp.float32)))


def s5_mixer(hn, w_in, a_re, a_im, log_dt, b_re, b_im, c_re, c_im, d_skip, w_glu):
    b, s, _ = hn.shape
    u = (hn @ w_in).astype(jnp.float32).reshape(b, s, SSM_GROUPS, SSM_GROUP_CH)
    y = d_skip.astype(jnp.float32).reshape(SSM_GROUPS, SSM_GROUP_CH) * u
    for direction in range(SSM_DIRS):
        y = y + s5_direction(u, a_re[direction], a_im[direction], log_dt[direction],
                             b_re[direction], b_im[direction], c_re[direction], c_im[direction],
                             reverse=(direction == 1))
    y = jax.nn.gelu(y.reshape(b, s, SSM_WIDTH)).astype(hn.dtype)
    ag = y @ w_glu
    return ag[..., :D_MODEL] * jax.nn.sigmoid(ag[..., D_MODEL:])


def swiglu_ffn(hn, w_in, w_out):
    gu = hn @ w_in
    return (jax.nn.silu(gu[..., :FFN_HIDDEN]) * gu[..., FFN_HIDDEN:]) @ w_out


def setup_inputs(seed: int = 0) -> dict:
    key = jax.random.key(seed)
    ks = jax.random.split(key, 22)
    f32 = jnp.float32
    na, ns = N_ATTN_LAYERS, N_SSM_LAYERS
    g, p, c = SSM_GROUPS, SSM_STATE, SSM_GROUP_CH
    resid = (2 * DEPTH) ** -0.5

    def normal(k, shape, scale):
        return jax.random.normal(k, shape, f32) * scale

    def gain(k, shape):
        return 1.0 + 0.05 * jax.random.normal(k, shape, f32)

    return {
        "x": normal(ks[0], (BATCH, SEQ, D_MODEL), 1.0),
        "attn_norm": gain(ks[1], (na, D_MODEL)),
        "w_qkv": normal(ks[2], (na, D_MODEL, QKV_WIDTH), D_MODEL ** -0.5),
        "w_attn_out": normal(ks[3], (na, ATTN_WIDTH, D_MODEL), ATTN_WIDTH ** -0.5 * resid),
        "ssm_norm": gain(ks[4], (ns, D_MODEL)),
        "w_ssm_in": normal(ks[5], (ns, D_MODEL, SSM_WIDTH), D_MODEL ** -0.5),
        "a_re": -0.5 + 0.05 * jax.random.uniform(ks[6], (ns, SSM_DIRS, g, p), f32, -1.0, 1.0),
        "a_im": math.pi * jnp.arange(p, dtype=f32) + 0.05 * normal(ks[7], (ns, SSM_DIRS, g, p), 1.0),
        "log_dt": jax.random.uniform(ks[8], (ns, SSM_DIRS, g), f32,
                                     math.log(SSM_DT_MIN), math.log(SSM_DT_MAX)),
        "b_re": normal(ks[9], (ns, SSM_DIRS, g, p, c), (2 * c) ** -0.5),
        "b_im": normal(ks[10], (ns, SSM_DIRS, g, p, c), (2 * c) ** -0.5),
        "c_re": normal(ks[11], (ns, SSM_DIRS, g, c, p), p ** -0.5),
        "c_im": normal(ks[12], (ns, SSM_DIRS, g, c, p), p ** -0.5),
        "d_skip": normal(ks[13], (ns, SSM_WIDTH), 1.0),
        "w_glu": normal(ks[14], (ns, SSM_WIDTH, 2 * D_MODEL), SSM_WIDTH ** -0.5 * resid),
        "ffn_norm": gain(ks[15], (DEPTH, D_MODEL)),
        "w_ffn_in": normal(ks[16], (DEPTH, D_MODEL, 2 * FFN_HIDDEN), D_MODEL ** -0.5),
        "w_ffn_out": normal(ks[17], (DEPTH, FFN_HIDDEN, D_MODEL), FFN_HIDDEN ** -0.5 * resid),
        "final_norm": gain(ks[18], (D_MODEL,)),
    }


def reference(x, attn_norm, w_qkv, w_attn_out, ssm_norm, w_ssm_in, a_re, a_im, log_dt,
              b_re, b_im, c_re, c_im, d_skip, w_glu, ffn_norm, w_ffn_in, w_ffn_out, final_norm):
    h = x
    for layer in range(DEPTH):
        slot = layer // N_MIXERS
        if layer % N_MIXERS == 0:
            h = h + dilated_attention_mixer(rms_norm(h, attn_norm[slot]), w_qkv[slot], w_attn_out[slot])
        else:
            h = h + s5_mixer(rms_norm(h, ssm_norm[slot]), w_ssm_in[slot], a_re[slot], a_im[slot],
                             log_dt[slot], b_re[slot], b_im[slot], c_re[slot], c_im[slot],
                             d_skip[slot], w_glu[slot])
        h = h + swiglu_ffn(rms_norm(h, ffn_norm[layer]), w_ffn_in[layer], w_ffn_out[layer])
    return rms_norm(h, final_norm)
```

```python
import functools
import math

import jax
import jax.numpy as jnp
from jax import lax
from jax.experimental import pallas as pl
from jax.experimental.pallas import tpu as pltpu

F32 = jnp.float32
BF16 = jnp.bfloat16

D_MODEL = 1024
HEAD_DIM = 64
ATTN_HEADS = D_MODEL // HEAD_DIM
ATTN_GROUPS = ((128, 1), (512, 4), (2048, 16))
N_ATTN_GROUPS = len(ATTN_GROUPS)
ATTN_WIDTH = ATTN_HEADS * HEAD_DIM
ALIBI_MAX_BIAS = 8.0
NEG_INF = -1e30
SSM_GROUP_CH = 16
SSM_GROUPS = D_MODEL // SSM_GROUP_CH
SSM_STATE = 64
FFN_HIDDEN = -(-8 * D_MODEL // (3 * 256)) * 256
NORM_EPS = 1e-6
LOG2_E = math.log2(math.e)

LANES = 128
HEAD_PAIRS = ATTN_WIDTH // LANES
ROW_TILE = 512
ATTN_HEAD_STEPS = 2
ATTN_TQ = 128
ATTN_ORDER_WIDE = (("q", 0), ("q", 1), ("s", 0), ("v", 0), ("q", 2), ("s", 1), ("v", 1), ("q", 3),
                   ("s", 2), ("v", 2), ("s", 3), ("v", 3))
ATTN_ORDER_NARROW = tuple((name, u) for name in "qsv" for u in range(4))
STAT_MAX_LANE = 0
STAT_DEN_EVEN_LANE = 64
STAT_DEN_ODD_LANE = 32
SSM_CHUNK = 16
FFN_CHUNK = 256
QKV_TILE = 1024
QKV_SUBTILES = 2
SSM_OUT_TILE = 1024
SSM_OUT_SUBTILES = 4
VMEM_LIMIT = 56 * 1024 * 1024


def _cparams(*sem):
    return pltpu.CompilerParams(dimension_semantics=sem, vmem_limit_bytes=VMEM_LIMIT)


def _rms_scale(x, gain):
    ms = jnp.mean(x * x, axis=-1, keepdims=True)
    return x * lax.rsqrt(ms + NORM_EPS) * gain


def _qkv_kernel(x_ref, gain_ref, w_ref, o_ref, *scratch, dilation):
    nsub = len(scratch) // 2
    tm = x_ref.shape[0]
    span = tm // nsub
    rows = span // dilation

    def gather(sub):
        xs_ref, xn_ref = scratch[2 * sub], scratch[2 * sub + 1]
        xn = _rms_scale(x_ref[sub * span:(sub + 1) * span, :], gain_ref[...])
        if dilation == 1:
            xn_ref[...] = xn.astype(BF16)
            return
        for j in range(D_MODEL // LANES):
            xs_ref[j] = xn[:, j * LANES:(j + 1) * LANES]
        for r in range(dilation):
            for j in range(D_MODEL // LANES):
                piece = xs_ref[j, pl.ds(r, rows, stride=dilation), :]
                xn_ref[r * rows:(r + 1) * rows, j * LANES:(j + 1) * LANES] = piece.astype(BF16)

    def project(sub):
        xn = scratch[2 * sub + 1][...]
        for n in range(3):
            acc = jnp.dot(xn, w_ref[:, n * ATTN_WIDTH:(n + 1) * ATTN_WIDTH], preferred_element_type=F32)
            if n == 0:
                acc = acc * (LOG2_E * HEAD_DIM ** -0.5)
            o_ref[0, :, sub * rows:(sub + 1) * rows, n * ATTN_WIDTH:(n + 1) * ATTN_WIDTH] = (
                acc.astype(BF16).reshape(dilation, rows, ATTN_WIDTH))

    gather(0)
    for sub in range(nsub):
        if sub + 1 < nsub:
            gather(sub + 1)
        project(sub)


def _qkv_project(x2, seq, gain, w_all, group, dilation):
    t = x2.shape[0]
    tm = QKV_TILE
    sub_rows = tm // QKV_SUBTILES
    per_seq = seq // tm
    assert seq % tm == 0 and sub_rows % (16 * dilation) == 0
    return pl.pallas_call(
        functools.partial(_qkv_kernel, dilation=dilation),
        grid=(t // tm,),
        in_specs=[
            pl.BlockSpec((tm, D_MODEL), lambda i: (i, 0)),
            pl.BlockSpec((1, D_MODEL), lambda i: (0, 0)),
            pl.BlockSpec((D_MODEL, 3 * ATTN_WIDTH), lambda i: (0, group)),
        ],
        out_specs=pl.BlockSpec((1, dilation, tm // dilation, 3 * ATTN_WIDTH),
                               lambda i: (i // per_seq, 0, i % per_seq, 0)),
        out_shape=jax.ShapeDtypeStruct((t // seq, dilation, seq // dilation, 3 * ATTN_WIDTH), BF16),
        scratch_shapes=[pltpu.VMEM((D_MODEL // LANES, sub_rows, LANES), F32),
                        pltpu.VMEM((sub_rows, D_MODEL), BF16)] * QKV_SUBTILES,
        compiler_params=_cparams("parallel"),
        name="qkv_proj_d%d" % dilation,
    )(x2, gain.reshape(1, D_MODEL), w_all)


def _attn_kernel(slopes_ref, q_ref, k_ref, v_ref, o_ref, lse_ref, bias_ref, *, dilation, length, tq, tk):
    pairs = q_ref.shape[3] // LANES
    half = tq // 2
    nblk = length // tq
    order = ATTN_ORDER_NARROW if tk == tq else ATTN_ORDER_WIDE
    per_iter = 1 + max(u for _, u in order)
    head_base = pl.program_id(1) * (2 * pairs)
    lane = lax.broadcasted_iota(jnp.int32, (tq, LANES), 1)
    lo_half = lane < HEAD_DIM
    head_mask = (jnp.where(lo_half, 1.0, 0.0).astype(BF16), jnp.where(lo_half, 0.0, 1.0).astype(BF16))
    key_lo = jnp.where(lax.broadcasted_iota(jnp.int32, (tk, LANES), 1) < HEAD_DIM, 1.0, 0.0).astype(BF16) > 0
    ones_kv = jnp.ones((tk, LANES), BF16)
    qq = lax.broadcasted_iota(jnp.int32, (tq, tk), 0)
    kk = lax.broadcasted_iota(jnp.int32, (tq, tk), 1)
    offsets = (0, -half, tq - tk)
    for c, off in enumerate(offsets[:bias_ref.shape[0]]):
        dist = jnp.abs(kk - qq + off)
        negdist = -(dist * dilation).astype(F32)
        for h in range(2 * pairs):
            slope2 = slopes_ref[head_base + h] * LOG2_E
            bias_ref[c, h] = jnp.where(dist <= half, slope2 * negdist, NEG_INF)

    def locate(idx):
        r = idx // nblk
        q0 = pl.multiple_of((idx % nblk) * tq, tq)
        ws = pl.multiple_of(jnp.clip(q0 - half, 0, length - tk), half)
        case = jnp.where(q0 == 0, 0, jnp.where(q0 - half > length - tk, 2, 1))
        row0 = r + q0 * dilation
        if dilation == 1:
            rows = pl.ds(pl.multiple_of(row0, tq), tq)
        else:
            rows = pl.ds(row0, tq, stride=dilation)
        return r, q0, ws, case, rows

    def blocks(it, carry):
        where = [locate(it * per_iter + u) for u in range(per_iter)]
        scores, probs, stats = {}, {}, {}

        def score_pass(u):
            r, q0, ws, case, rows = where[u]
            for p in range(pairs):
                cols = slice(p * LANES, (p + 1) * LANES)
                q = q_ref[0, r, pl.ds(q0, tq), cols]
                k = k_ref[0, r, pl.ds(ws, tk), cols]
                for hh in range(2):
                    scores[u, 2 * p + hh] = lax.dot_general(q * head_mask[hh], k, (((1,), (1,)), ((), ())),
                                                            preferred_element_type=F32)

        def softmax_pass(u):
            case = where[u][3]
            stat = jnp.zeros((tq, LANES), F32)
            for h in range(2 * pairs):
                s = scores.pop((u, h)) + bias_ref[case, h]
                m = jnp.max(s, axis=-1, keepdims=True)
                probs[u, h] = jnp.exp2(s - m).astype(BF16)
                stat = jnp.where(lane == STAT_MAX_LANE + h, m, stat)
            stats[u] = stat

        def value_pass(u):
            r, q0, ws, case, rows = where[u]
            stat = stats[u]
            for p in range(pairs):
                v = v_ref[0, r, pl.ds(ws, tk), p * LANES:(p + 1) * LANES]
                v_aug = (jnp.where(key_lo, v, ones_kv), jnp.where(key_lo, ones_kv, v))
                outs = [jnp.dot(probs.pop((u, 2 * p + hh)), v_aug[hh], preferred_element_type=F32)
                        for hh in range(2)]
                o_ref[0, p, rows, :] = jnp.where(lo_half, outs[0], outs[1])
                stat = jnp.where(lane == STAT_DEN_EVEN_LANE + 2 * p, outs[0], stat)
                stat = jnp.where(lane == STAT_DEN_ODD_LANE + 2 * p + 1, outs[1], stat)
            lse_ref[0, 0, rows, :] = stat

        passes = {"q": score_pass, "s": softmax_pass, "v": value_pass}
        for name, u in order:
            passes[name](u)
        return carry

    assert (dilation * nblk) % per_iter == 0
    lax.fori_loop(0, dilation * nblk // per_iter, blocks, 0)


def _banded_attention(qkv, slopes, dilation):
    b, _, length, _ = qkv.shape
    s = dilation * length
    tq = ATTN_TQ
    tk = min(2 * tq, length)
    pairs = HEAD_PAIRS // ATTN_HEAD_STEPS
    width = pairs * LANES
    nw = ATTN_WIDTH // width
    blk = (1, dilation, length, width)
    return pl.pallas_call(
        functools.partial(_attn_kernel, dilation=dilation, length=length, tq=tq, tk=tk),
        grid=(b, ATTN_HEAD_STEPS),
        in_specs=[
            pl.BlockSpec(memory_space=pltpu.SMEM),
            pl.BlockSpec(blk, lambda i, j: (i, 0, 0, j)),
            pl.BlockSpec(blk, lambda i, j: (i, 0, 0, nw + j)),
            pl.BlockSpec(blk, lambda i, j: (i, 0, 0, 2 * nw + j)),
        ],
        out_specs=[
            pl.BlockSpec((1, pairs, s, LANES), lambda i, j: (i, j, 0, 0)),
            pl.BlockSpec((1, 1, s, LANES), lambda i, j: (i, j, 0, 0)),
        ],
        out_shape=[
            jax.ShapeDtypeStruct((b, HEAD_PAIRS, s, LANES), F32),
            jax.ShapeDtypeStruct((b, ATTN_HEAD_STEPS, s, LANES), F32),
        ],
        scratch_shapes=[pltpu.VMEM((1 if length == tq else 3, 2 * pairs, tq, tk), F32)],
        compiler_params=_cparams("parallel", "arbitrary"),
        name="band_attn_d%d" % dilation,
    )(slopes, qkv, qkv, qkv)


def _resident(shape):
    return pl.BlockSpec(shape, lambda i: (0,) * len(shape), pipeline_mode=pl.Buffered(1))


def _ffn_steps(make_h, gain_ref, w_in_ref, w_out_ref, acc_ref):
    state = {}

    def start():
        h = make_h()
        state["hn"] = _rms_scale(h, gain_ref[...]).astype(BF16)
        acc_ref[...] = h

    def chunk(c0):
        hn = state["hn"]
        a = jnp.dot(hn, w_in_ref[:, c0:c0 + FFN_CHUNK], preferred_element_type=F32)
        g = jnp.dot(hn, w_in_ref[:, FFN_HIDDEN + c0:FFN_HIDDEN + c0 + FFN_CHUNK], preferred_element_type=F32)
        act = (a * (1.0 / (1.0 + jnp.exp(-a))) * g).astype(BF16)
        acc_ref[...] += jnp.dot(act, w_out_ref[c0:c0 + FFN_CHUNK, :], preferred_element_type=F32)

    return [start] + [functools.partial(chunk, c0) for c0 in range(0, FFN_HIDDEN, FFN_CHUNK)]


def _run_interleaved(primary, secondary):
    done = 0
    for i, step in enumerate(primary):
        step()
        upto = (i + 1) * len(secondary) // len(primary)
        for extra in secondary[done:upto]:
            extra()
        done = upto


def _attn_tail_kernel(x_ref, o0_ref, o1_ref, o2_ref, l0_ref, l1_ref, l2_ref, w_ref,
                      gain_ref, w_in_ref, w_out_ref, out_ref, *scratch):
    o_refs = (o0_ref, o1_ref, o2_ref)
    l_refs = (l0_ref, l1_ref, l2_ref)
    cur_ref, next_ref, acc_ref = scratch
    rows = x_ref.shape[0]
    steps = l0_ref.shape[1]
    pairs = HEAD_PAIRS // steps
    lane = lax.broadcasted_iota(jnp.int32, (rows, LANES), 1)
    even_head = (lane & 1) == 0
    is_head = lane < 2 * pairs
    lo_half = lane < HEAD_DIM

    @pl.when(pl.program_id(0) == 0)
    def _():
        next_ref[...] = jnp.zeros_like(next_ref)

    cur_ref[...] = next_ref[...]

    state = {}

    def weights(st):
        stats = [l_refs[g][0, st] for g in range(N_ATTN_GROUPS)]
        dens = [jnp.where(even_head, pltpu.roll(s, LANES - STAT_DEN_EVEN_LANE, 1),
                          pltpu.roll(s, LANES - STAT_DEN_ODD_LANE, 1)) for s in stats]
        mx = jnp.maximum(jnp.maximum(stats[0], stats[1]), stats[2])
        es = [jnp.exp2(s - mx) for s in stats]
        tot = es[0] * dens[0] + es[1] * dens[1] + es[2] * dens[2]
        state["w"] = [jnp.where(is_head, e / tot, 0.0) for e in es]
        state["acc"] = [None] * pairs

    def add_group(st, g):
        w = state["w"][g]
        acc = state["acc"]
        for p in range(pairs):
            coef = jnp.where(lo_half, jnp.broadcast_to(w[:, 2 * p:2 * p + 1], (rows, LANES)),
                             jnp.broadcast_to(w[:, 2 * p + 1:2 * p + 2], (rows, LANES)))
            term = coef * o_refs[g][0, st * pairs + p]
            acc[p] = term if acc[p] is None else acc[p] + term

    def store(st):
        for p in range(pairs):
            c0 = (st * pairs + p) * LANES
            next_ref[:, c0:c0 + LANES] = state["acc"][p].astype(BF16)

    merge_steps = []
    for st in range(steps):
        merge_steps.append(functools.partial(weights, st))
        merge_steps += [functools.partial(add_group, st, g) for g in range(N_ATTN_GROUPS)]
        merge_steps.append(functools.partial(store, st))

    def project():
        return x_ref[...] + jnp.dot(cur_ref[...], w_ref[...], preferred_element_type=F32)

    _run_interleaved(_ffn_steps(project, gain_ref, w_in_ref, w_out_ref, acc_ref), merge_steps)
    out_ref[...] = acc_ref[...]


def _attn_tail(x2, outs, lses, w_out, gain, w_ffn_in, w_ffn_out, seq):
    t = x2.shape[0]
    tm = ROW_TILE
    n = t // tm
    per_seq = seq // tm
    steps = lses[0].shape[1]
    row_spec = pl.BlockSpec((tm, D_MODEL), lambda s: (jnp.maximum(s - 1, 0), 0))

    def group_map(s):
        i = jnp.minimum(s, n - 1)
        return (i // per_seq, 0, i % per_seq, 0)
    o_spec = pl.BlockSpec((1, HEAD_PAIRS, tm, LANES), group_map)
    l_spec = pl.BlockSpec((1, steps, tm, LANES), group_map)
    return pl.pallas_call(
        _attn_tail_kernel,
        grid=(n + 1,),
        in_specs=[row_spec, o_spec, o_spec, o_spec, l_spec, l_spec, l_spec,
                  _resident((ATTN_WIDTH, D_MODEL)),
                  _resident((1, D_MODEL)), _resident((D_MODEL, 2 * FFN_HIDDEN)), _resident((FFN_HIDDEN, D_MODEL))],
        out_specs=row_spec,
        out_shape=jax.ShapeDtypeStruct((t, D_MODEL), F32),
        scratch_shapes=[pltpu.VMEM((tm, ATTN_WIDTH), BF16), pltpu.VMEM((tm, ATTN_WIDTH), BF16),
                        pltpu.VMEM((tm, D_MODEL), F32)],
        compiler_params=_cparams("arbitrary"),
        name="attn_merge_ffn",
    )(x2, *outs, *lses, w_out, gain.reshape(1, D_MODEL), w_ffn_in, w_ffn_out)


GROUPS_PER_SLAB = LANES // SSM_GROUP_CH


def _swap_lane_groups(arrs):
    lane_group = lax.broadcasted_iota(jnp.int32, arrs[0].shape, 1) // SSM_GROUP_CH
    arrs = list(arrs)
    for k in (4, 2, 1):
        upper = (lane_group & k) != 0
        for i in range(len(arrs)):
            if i & k:
                continue
            lo, hi = arrs[i], arrs[i + k]
            arrs[i] = jnp.where(upper, pltpu.roll(hi, SSM_GROUP_CH * k, 1), lo)
            arrs[i + k] = jnp.where(upper, hi, pltpu.roll(lo, LANES - SSM_GROUP_CH * k, 1))
    return arrs


def _ssm_in_kernel(h_ref, gain_ref, w_ref, u_ref, a_ref, *us_refs):
    nsub = len(us_refs)
    tm = h_ref.shape[0]
    rows = tm // nsub
    nq = rows // SSM_CHUNK

    def project(sub):
        span = slice(sub * rows, (sub + 1) * rows)
        hn = _rms_scale(h_ref[span, :], gain_ref[...]).astype(BF16)
        u = jnp.dot(hn, w_ref[...], preferred_element_type=F32)
        u_ref[span, :] = u
        for j in range(D_MODEL // LANES):
            us_refs[sub][j] = u[:, j * LANES:(j + 1) * LANES]

    def regroup(sub):
        us_ref = us_refs[sub]
        for j in range(D_MODEL // LANES):
            for part in range(SSM_CHUNK // GROUPS_PER_SLAB):
                steps = [us_ref[j, pl.ds(part * GROUPS_PER_SLAB + i, nq, stride=SSM_CHUNK), :]
                         for i in range(GROUPS_PER_SLAB)]
                per_group = _swap_lane_groups(steps)
                for g in range(GROUPS_PER_SLAB):
                    a_ref[j * GROUPS_PER_SLAB + g, sub * nq:(sub + 1) * nq, part * LANES:(part + 1) * LANES] = (
                        per_group[g].astype(BF16))

    project(0)
    for sub in range(nsub):
        if sub + 1 < nsub:
            project(sub + 1)
        regroup(sub)


def _ssm_in_project(h2, gain, w_in):
    t = h2.shape[0]
    tm = SSM_OUT_TILE
    sub_rows = tm // SSM_OUT_SUBTILES
    width = SSM_CHUNK * SSM_GROUP_CH
    return pl.pallas_call(
        _ssm_in_kernel,
        grid=(t // tm,),
        in_specs=[pl.BlockSpec((tm, D_MODEL), lambda i: (i, 0)),
                  pl.BlockSpec((1, D_MODEL), lambda i: (0, 0)),
                  pl.BlockSpec((D_MODEL, D_MODEL), lambda i: (0, 0))],
        out_specs=[pl.BlockSpec((tm, D_MODEL), lambda i: (i, 0)),
                   pl.BlockSpec((SSM_GROUPS, tm // SSM_CHUNK, width), lambda i: (0, i, 0))],
        out_shape=[jax.ShapeDtypeStruct((t, D_MODEL), F32),
                   jax.ShapeDtypeStruct((SSM_GROUPS, t // SSM_CHUNK, width), BF16)],
        scratch_shapes=[pltpu.VMEM((D_MODEL // LANES, sub_rows, LANES), F32)] * SSM_OUT_SUBTILES,
        compiler_params=_cparams("parallel"),
        name="ssm_in_proj",
    )(h2, gain.reshape(1, D_MODEL), w_in)


def _ssm_operators(a_re, a_im, log_dt, b_re, b_im, c_re, c_im):
    hp = lax.Precision.HIGHEST
    nl = SSM_CHUNK
    dt = jnp.exp(log_dt)[..., None]
    mag = jnp.exp(a_re * dt)
    lr, li = mag * jnp.cos(a_im * dt), mag * jnp.sin(a_im * dt)
    inv = 1.0 / (a_re * a_re + a_im * a_im)
    fr = ((lr - 1.0) * a_re + li * a_im) * inv
    fi = (li * a_re - (lr - 1.0) * a_im) * inv
    bbr = fr[..., None] * b_re - fi[..., None] * b_im
    bbi = fr[..., None] * b_im + fi[..., None] * b_re
    kpow = jnp.arange(nl + 1, dtype=F32)[:, None, None, None]
    pmag = jnp.exp(kpow * (a_re * dt)[None])
    pr = pmag * jnp.cos(kpow * (a_im * dt)[None])
    pi = pmag * jnp.sin(kpow * (a_im * dt)[None])

    width = nl * SSM_GROUP_CH
    seq_r = jnp.stack([pr[:, 0], pr[::-1, 1]], axis=0).transpose(0, 2, 1, 3)
    seq_i = jnp.stack([pi[:, 0], pi[::-1, 1]], axis=0).transpose(0, 2, 1, 3)
    seq_r = jnp.repeat(seq_r, SSM_GROUP_CH, axis=2)
    seq_i = jnp.repeat(seq_i, SSM_GROUP_CH, axis=2)
    ct_r = jnp.tile(c_re, (1, 1, nl + 1, 1))
    ct_i = jnp.tile(c_im, (1, 1, nl + 1, 1))
    w_r = ct_r * seq_r - ct_i * seq_i
    w_i = ct_r * seq_i + ct_i * seq_r

    def from_state(d, lo):
        wr = w_r[d, :, lo:lo + width].transpose(0, 2, 1)
        wi = -w_i[d, :, lo:lo + width].transpose(0, 2, 1)
        return wr, wi
    ff_r, ff_i = from_state(0, SSM_GROUP_CH)
    fb_r, fb_i = from_state(1, 0)
    from_st = jnp.concatenate([ff_r, fb_r, ff_i, fb_i], axis=1)

    def lag_kernels(d, lo):
        return (jnp.einsum('gjp,gpe->gej', w_r[d, :, lo:lo + width], bbr[d], precision=hp)
                - jnp.einsum('gjp,gpe->gej', w_i[d, :, lo:lo + width], bbi[d], precision=hp))
    lag_f = jnp.pad(lag_kernels(0, 0), ((0, 0), (0, 0), (width - SSM_GROUP_CH, 0)))
    lag_b = jnp.pad(lag_kernels(1, SSM_GROUP_CH), ((0, 0), (0, 0), (0, width - SSM_GROUP_CH)))
    rows = []
    for s in range(nl):
        f0 = width - SSM_GROUP_CH - SSM_GROUP_CH * s
        b0 = SSM_GROUP_CH * (nl - 1 - s)
        rows.append(lag_f[:, :, f0:f0 + width] + lag_b[:, :, b0:b0 + width])
    toep = jnp.stack(rows, axis=1).reshape(SSM_GROUPS, width, width)

    pw_r = jnp.stack([pr[:nl, 0][::-1], pr[:nl, 1]], axis=0).transpose(0, 2, 1, 3)
    pw_i = jnp.stack([pi[:nl, 0][::-1], pi[:nl, 1]], axis=0).transpose(0, 2, 1, 3)
    pw_r = jnp.repeat(pw_r, SSM_GROUP_CH, axis=2)
    pw_i = jnp.repeat(pw_i, SSM_GROUP_CH, axis=2)
    bt_r = jnp.tile(bbr.transpose(0, 1, 3, 2), (1, 1, nl, 1))
    bt_i = jnp.tile(bbi.transpose(0, 1, 3, 2), (1, 1, nl, 1))
    st_r = pw_r * bt_r - pw_i * bt_i
    st_i = pw_r * bt_i + pw_i * bt_r
    to_st = jnp.concatenate([st_r[0], st_r[1], st_i[0], st_i[1]], axis=-1)

    lam_r = jnp.concatenate([pr[nl, 0], pr[nl, 1]], axis=-1)
    lam_i = jnp.concatenate([pi[nl, 0], pi[nl, 1]], axis=-1)
    lam = jnp.stack([lam_r, lam_i], axis=1)
    return toep.astype(BF16), to_st.astype(BF16), from_st.astype(BF16), lam


SCAN_PAD_ROWS = 8


def _ssm_core_kernel(a_ref, toep_ref, tost_ref, fromst_ref, lam_ref, y_ref,
                     sr_ref, si_ref, fr_ref, fi_ref, br_ref, bi_ref, *, nb, nchunks):
    pitch = nchunks + SCAN_PAD_ROWS
    a = a_ref[0]
    sc = jnp.dot(a, tost_ref[0], preferred_element_type=F32)
    pad = jnp.zeros((SCAN_PAD_ROWS, LANES), F32)
    for b in range(nb):
        sr_ref[b * pitch:b * pitch + nchunks, :] = sc[b * nchunks:(b + 1) * nchunks, 0:LANES]
        si_ref[b * pitch:b * pitch + nchunks, :] = sc[b * nchunks:(b + 1) * nchunks, LANES:2 * LANES]
        for ref in (fr_ref, fi_ref, br_ref, bi_ref):
            ref[b * pitch + nchunks:(b + 1) * pitch, :] = pad
    lam_r = jnp.broadcast_to(lam_ref[0, 0:1, :], (nb, LANES))
    lam_i = jnp.broadcast_to(lam_ref[0, 1:2, :], (nb, LANES))
    fwd_lane = lax.broadcasted_iota(jnp.int32, (nb, LANES), 1) < SSM_STATE

    def step(j, carry):
        xr, xi = carry
        rows_f = pl.ds(j, nb, stride=pitch)
        rows_b = pl.ds(nchunks - 1 - j, nb, stride=pitch)
        fr_ref[rows_f, :] = xr
        fi_ref[rows_f, :] = xi
        br_ref[rows_b, :] = xr
        bi_ref[rows_b, :] = xi
        sr = jnp.where(fwd_lane, sr_ref[rows_f, :], sr_ref[rows_b, :])
        si = jnp.where(fwd_lane, si_ref[rows_f, :], si_ref[rows_b, :])
        return (lam_r * xr - lam_i * xi + sr, lam_r * xi + lam_i * xr + si)

    zero = jnp.zeros((nb, LANES), F32)
    lax.fori_loop(0, nchunks, step, (zero, zero), unroll=4)
    fwd_all = lax.broadcasted_iota(jnp.int32, (nb * pitch, LANES), 1) < SSM_STATE
    carried = jnp.concatenate([jnp.where(fwd_all, fr_ref[...], br_ref[...]),
                               jnp.where(fwd_all, fi_ref[...], bi_ref[...])], axis=1).astype(BF16)
    z = jnp.dot(carried, fromst_ref[0], preferred_element_type=F32)
    y = jnp.dot(a, toep_ref[0], preferred_element_type=F32)
    for b in range(nb):
        y_ref[0, b * nchunks:(b + 1) * nchunks, :] = y[b * nchunks:(b + 1) * nchunks] + z[b * pitch:b * pitch + nchunks]


def _ssm_core(a, toep, to_st, from_st, lam, nb):
    g, rows, width = a.shape
    nchunks = rows // nb
    op_spec = pl.BlockSpec((1, width, width), lambda i: (i, 0, 0))
    state = pltpu.VMEM((nb * (nchunks + SCAN_PAD_ROWS), LANES), F32)
    return pl.pallas_call(
        functools.partial(_ssm_core_kernel, nb=nb, nchunks=nchunks),
        grid=(g,),
        in_specs=[pl.BlockSpec((1, rows, width), lambda i: (i, 0, 0)), op_spec, op_spec, op_spec,
                  pl.BlockSpec((1, 2, LANES), lambda i: (i, 0, 0))],
        out_specs=pl.BlockSpec((1, rows, width), lambda i: (i, 0, 0)),
        out_shape=jax.ShapeDtypeStruct((g, rows, width), F32),
        scratch_shapes=[state] * 6,
        compiler_params=_cparams("parallel"),
        name="ssm_chunk_scan",
    )(a, toep, to_st, from_st, lam)


def _ssm_out_kernel(h_ref, y_ref, u_ref, skip_ref, w_ref, out_ref, *scratch):
    nsub = len(scratch) // 2
    tm = h_ref.shape[0]
    rows = tm // nsub
    nq = rows // SSM_CHUNK

    def activate(sub):
        ys_ref, act_ref = scratch[2 * sub], scratch[2 * sub + 1]
        for j in range(D_MODEL // LANES):
            for part in range(SSM_CHUNK // GROUPS_PER_SLAB):
                per_group = [y_ref[j * GROUPS_PER_SLAB + g, sub * nq:(sub + 1) * nq, part * LANES:(part + 1) * LANES]
                             for g in range(GROUPS_PER_SLAB)]
                steps = _swap_lane_groups(per_group)
                for i in range(GROUPS_PER_SLAB):
                    ys_ref[j, pl.ds(part * GROUPS_PER_SLAB + i, nq, stride=SSM_CHUNK), :] = steps[i]
        for j in range(D_MODEL // LANES):
            cols = slice(j * LANES, (j + 1) * LANES)
            y = ys_ref[j] + skip_ref[:, cols] * u_ref[sub * rows:(sub + 1) * rows, cols]
            cdf = 0.5 * (1.0 + jnp.tanh(math.sqrt(2.0 / math.pi) * (y + 0.044715 * (y * y * y))))
            act_ref[:, cols] = (y * cdf).astype(BF16)

    def project(sub):
        act = scratch[2 * sub + 1][...]
        span = slice(sub * rows, (sub + 1) * rows)
        lin = jnp.dot(act, w_ref[:, 0:D_MODEL], preferred_element_type=F32)
        gate = jnp.dot(act, w_ref[:, D_MODEL:2 * D_MODEL], preferred_element_type=F32)
        out_ref[span, :] = h_ref[span, :] + lin * (1.0 / (1.0 + jnp.exp(-gate)))

    activate(0)
    for sub in range(nsub):
        if sub + 1 < nsub:
            activate(sub + 1)
        project(sub)


def _ssm_out(h2, y, u2, d_skip, w_glu):
    t = h2.shape[0]
    tm = SSM_OUT_TILE
    sub_rows = tm // SSM_OUT_SUBTILES
    row_spec = pl.BlockSpec((tm, D_MODEL), lambda i: (i, 0))
    return pl.pallas_call(
        _ssm_out_kernel,
        grid=(t // tm,),
        in_specs=[row_spec,
                  pl.BlockSpec((SSM_GROUPS, tm // SSM_CHUNK, y.shape[2]), lambda i: (0, i, 0)),
                  row_spec,
                  _resident((1, D_MODEL)), _resident((D_MODEL, 2 * D_MODEL))],
        out_specs=row_spec,
        out_shape=jax.ShapeDtypeStruct((t, D_MODEL), F32),
        scratch_shapes=[pltpu.VMEM((D_MODEL // LANES, sub_rows, LANES), F32),
                        pltpu.VMEM((sub_rows, D_MODEL), BF16)] * SSM_OUT_SUBTILES,
        compiler_params=_cparams("parallel"),
        name="ssm_glu_out",
    )(h2, y, u2, d_skip.reshape(1, D_MODEL), w_glu)


def _final_ffn_kernel(h_ref, gain_ref, w_in_ref, w_out_ref, fgain_ref, out_ref, acc_ref):
    for step in _ffn_steps(lambda: h_ref[...], gain_ref, w_in_ref, w_out_ref, acc_ref):
        step()
    out_ref[...] = _rms_scale(acc_ref[...], fgain_ref[...])


def _final_ffn(h2, gain, w_in, w_out, final_gain):
    t = h2.shape[0]
    tm = ROW_TILE
    row_spec = pl.BlockSpec((tm, D_MODEL), lambda i: (i, 0))
    return pl.pallas_call(
        _final_ffn_kernel,
        grid=(t // tm,),
        in_specs=[row_spec, _resident((1, D_MODEL)), _resident((D_MODEL, 2 * FFN_HIDDEN)),
                  _resident((FFN_HIDDEN, D_MODEL)), _resident((1, D_MODEL))],
        out_specs=row_spec,
        out_shape=jax.ShapeDtypeStruct((t, D_MODEL), F32),
        scratch_shapes=[pltpu.VMEM((tm, D_MODEL), F32)],
        compiler_params=_cparams("parallel"),
        name="swiglu_ffn_final",
    )(h2, gain.reshape(1, D_MODEL), w_in, w_out, final_gain.reshape(1, D_MODEL))


def _attention_layer(h2, batch, seq, gain, w_qkv, w_out, ffn_gain, w_ffn_in, w_ffn_out):
    slopes = 2.0 ** (-ALIBI_MAX_BIAS * jnp.arange(1, ATTN_HEADS + 1, dtype=F32) / ATTN_HEADS)
    w_bf = w_qkv.astype(BF16)
    outs, lses = [], []
    for g, (window, dilation) in enumerate(ATTN_GROUPS):
        assert window // 2 // dilation == ATTN_TQ // 2 and (seq // dilation) % ATTN_TQ == 0
        qkv = _qkv_project(h2, seq, gain, w_bf, g, dilation)
        o, lse = _banded_attention(qkv, slopes, dilation)
        outs.append(o)
        lses.append(lse)
    return _attn_tail(h2, outs, lses, w_out.astype(BF16), ffn_gain, w_ffn_in.astype(BF16),
                      w_ffn_out.astype(BF16), seq)


def _s5_layer(h2, batch, seq, gain, w_in, a_re, a_im, log_dt, b_re, b_im, c_re, c_im, d_skip, w_glu,
              ffn_gain, w_ffn_in, w_ffn_out, final_gain):
    u2, a = _ssm_in_project(h2, gain, w_in.astype(BF16))
    toep, to_st, from_st, lam = _ssm_operators(a_re, a_im, log_dt, b_re, b_im, c_re, c_im)
    y = _ssm_core(a, toep, to_st, from_st, lam, batch)
    h3 = _ssm_out(h2, y, u2, d_skip, w_glu.astype(BF16))
    return _final_ffn(h3, ffn_gain, w_ffn_in.astype(BF16), w_ffn_out.astype(BF16), final_gain)


def kernel(x, attn_norm, w_qkv, w_attn_out, ssm_norm, w_ssm_in, a_re, a_im, log_dt, b_re, b_im, c_re, c_im,
           d_skip, w_glu, ffn_norm, w_ffn_in, w_ffn_out, final_norm):
    batch, seq, _ = x.shape
    h = x.reshape(batch * seq, D_MODEL)
    h = _attention_layer(h, batch, seq, attn_norm[0], w_qkv[0], w_attn_out[0],
                         ffn_norm[0], w_ffn_in[0], w_ffn_out[0])
    h = _s5_layer(h, batch, seq, ssm_norm[0], w_ssm_in[0], a_re[0], a_im[0], log_dt[0], b_re[0], b_im[0],
                  c_re[0], c_im[0], d_skip[0], w_glu[0], ffn_norm[1], w_ffn_in[1], w_ffn_out[1], final_norm)
    return h.reshape(batch, seq, D_MODEL)
```
